```python
import jax, jax.numpy as jnp
from jax import lax
import numpy as np

D_MODEL = 2048
BATCH = 2
SEQ = 16384
DEPTH = 1

CHUNK = 64

D_MIX = D_MODEL
D_LRU = D_MIX // 2
D_POOL = D_MIX - D_LRU
LRU_HEADS = 16
LRU_HEAD_DIM = D_LRU // LRU_HEADS
CONV_WIDTH = 4
LRU_C = 8.0
POOL_WINDOWS = (2, 4, 8, 16)
POOL_GROUPS = len(POOL_WINDOWS)
POOL_GROUP_DIM = D_POOL // POOL_GROUPS
D_FF = ((8 * D_MODEL + 3 * 256 - 1) // (3 * 256)) * 256
N_MOD = 6
EPS = 1e-6

kernel_name = "hybrid_rglru_pool_swiglu_adaln"


def rmsnorm(x, g):
    xf = x.astype(jnp.float32)
    y = xf * lax.rsqrt(jnp.mean(xf * xf, axis=-1, keepdims=True) + EPS)
    return (y * g.astype(jnp.float32)).astype(x.dtype)


def modulate(h, shift, scale):
    return h * (1 + scale[:, None, :]) + shift[:, None, :]


def causal_dwconv(x, w, b):
    y = lax.conv_general_dilated(
        x, w[:, None, :].astype(x.dtype), window_strides=(1,),
        padding=[(CONV_WIDTH - 1, 0)],
        dimension_numbers=("NWC", "WIO", "NWC"),
        feature_group_count=x.shape[-1])
    return y + b.astype(x.dtype)


def rg_lru(x, w_a, b_a, w_i, b_i, lam):
    bsz, seq, _ = x.shape
    xf = x.astype(jnp.float32)
    xh = xf.reshape(bsz, seq, LRU_HEADS, LRU_HEAD_DIM)
    r = jax.nn.sigmoid(jnp.einsum("bshi,hij->bshj", xh, w_a.astype(jnp.float32)).reshape(bsz, seq, D_LRU)
                       + b_a.astype(jnp.float32))
    i = jax.nn.sigmoid(jnp.einsum("bshi,hij->bshj", xh, w_i.astype(jnp.float32)).reshape(bsz, seq, D_LRU)
                       + b_i.astype(jnp.float32))
    log_a = LRU_C * r * jax.nn.log_sigmoid(lam.astype(jnp.float32))
    a = jnp.exp(log_a)
    mult = jnp.sqrt(-jnp.expm1(2.0 * log_a))
    u = mult * (i * xf)

    def combine(left, right):
        a1, b1 = left
        a2, b2 = right
        return a1 * a2, a2 * b1 + b2

    _, h = lax.associative_scan(combine, (a, u), axis=1)
    return h


def pool_mixer(x, w_pool, ls_pool):
    bsz, seq, _ = x.shape
    xf = x.astype(jnp.float32)
    cs0 = jnp.concatenate([jnp.zeros((bsz, 1, D_POOL), jnp.float32), jnp.cumsum(xf, axis=1)], axis=1)
    pos1 = jnp.arange(1, seq + 1, dtype=jnp.float32)
    outs = []
    for g, w in enumerate(POOL_WINDOWS):
        sl = slice(g * POOL_GROUP_DIM, (g + 1) * POOL_GROUP_DIM)
        c0 = cs0[..., sl]
        upper = c0[:, 1:]
        lower = jnp.concatenate([jnp.zeros((bsz, w - 1, POOL_GROUP_DIM), jnp.float32),
                                 c0[:, :seq - w + 1]], axis=1)
        count = jnp.minimum(pos1, float(w))[None, :, None]
        outs.append((upper - lower) / count - xf[..., sl])
    pooled = jnp.stack(outs, axis=2)
    y = jnp.einsum("bsgc,gcd->bsgd", pooled, w_pool.astype(jnp.float32)) * ls_pool.astype(jnp.float32)
    return y.reshape(bsz, seq, D_POOL)


def setup_inputs(seed: int = 0) -> dict:
    key = jax.random.key(seed)
    ks = jax.random.split(key, 24)
    f32 = jnp.float32
    nrm = lambda k, shape, s: jax.random.normal(k, shape, f32) * s
    a0 = jax.random.uniform(ks[12], (DEPTH, D_LRU), f32, 0.9, 0.999)
    s0 = a0 ** (1.0 / LRU_C)
    lam = jnp.log(s0) - jnp.log1p(-s0)
    return {
        "x": nrm(ks[0], (BATCH, SEQ, D_MODEL), 1.0),
        "c": nrm(ks[1], (BATCH, D_MODEL), 1.0),
        "w_ada": nrm(ks[2], (DEPTH, D_MODEL, N_MOD * D_MODEL), 0.5 * D_MODEL ** -0.5),
        "b_ada": nrm(ks[3], (DEPTH, N_MOD * D_MODEL), 0.02),
        "g_norm_mix": 1.0 + nrm(ks[4], (DEPTH, D_MODEL), 0.05),
        "w_in": nrm(ks[5], (DEPTH, D_MODEL, 2 * D_LRU + D_POOL), D_MODEL ** -0.5),
        "w_conv": nrm(ks[6], (DEPTH, CONV_WIDTH, D_LRU), CONV_WIDTH ** -0.5),
        "b_conv": nrm(ks[7], (DEPTH, D_LRU), 0.02),
        "w_rg_a": nrm(ks[8], (DEPTH, LRU_HEADS, LRU_HEAD_DIM, LRU_HEAD_DIM), LRU_HEAD_DIM ** -0.5),
        "b_rg_a": nrm(ks[9], (DEPTH, D_LRU), 0.02),
        "w_rg_i": nrm(ks[10], (DEPTH, LRU_HEADS, LRU_HEAD_DIM, LRU_HEAD_DIM), LRU_HEAD_DIM ** -0.5),
        "b_rg_i": nrm(ks[11], (DEPTH, D_LRU), 0.02),
        "lru_lambda": lam,
        "w_pool": nrm(ks[13], (DEPTH, POOL_GROUPS, POOL_GROUP_DIM, POOL_GROUP_DIM), POOL_GROUP_DIM ** -0.5),
        "ls_pool": 1.0 + nrm(ks[14], (DEPTH, POOL_GROUPS, POOL_GROUP_DIM), 0.1),
        "w_out": nrm(ks[15], (DEPTH, D_MIX, D_MODEL), D_MIX ** -0.5),
        "g_norm_ffn": 1.0 + nrm(ks[16], (DEPTH, D_MODEL), 0.05),
        "w_ffn_gate": nrm(ks[17], (DEPTH, D_MODEL, D_FF), D_MODEL ** -0.5),
        "w_ffn_up": nrm(ks[18], (DEPTH, D_MODEL, D_FF), D_MODEL ** -0.5),
        "w_ffn_down": nrm(ks[19], (DEPTH, D_FF, D_MODEL), D_FF ** -0.5),
        "g_norm_final": 1.0 + nrm(ks[20], (D_MODEL,), 0.05),
    }


def reference(x, c, w_ada, b_ada, g_norm_mix, w_in, w_conv, b_conv, w_rg_a, b_rg_a,
              w_rg_i, b_rg_i, lru_lambda, w_pool, ls_pool, w_out, g_norm_ffn,
              w_ffn_gate, w_ffn_up, w_ffn_down, g_norm_final):
    dt = x.dtype
    c_act = jax.nn.silu(c)
    for l in range(DEPTH):
        mod = c_act @ w_ada[l] + b_ada[l]
        sh1, sc1, gt1, sh2, sc2, gt2 = jnp.split(mod, N_MOD, axis=-1)

        h = modulate(rmsnorm(x, g_norm_mix[l]), sh1, sc1)
        proj = h @ w_in[l]
        xr = proj[..., :D_LRU]
        gr = proj[..., D_LRU:2 * D_LRU]
        xp = proj[..., 2 * D_LRU:]
        xr = causal_dwconv(xr, w_conv[l], b_conv[l])
        hr = rg_lru(xr, w_rg_a[l], b_rg_a[l], w_rg_i[l], b_rg_i[l], lru_lambda[l])
        y_lru = (hr * jax.nn.gelu(gr.astype(jnp.float32))).astype(dt)
        y_pool = pool_mixer(xp, w_pool[l], ls_pool[l]).astype(dt)
        y = jnp.concatenate([y_lru, y_pool], axis=-1) @ w_out[l]
        x = x + gt1[:, None, :] * y

        h = modulate(rmsnorm(x, g_norm_ffn[l]), sh2, sc2)
        f = (jax.nn.silu(h @ w_ffn_gate[l]) * (h @ w_ffn_up[l])) @ w_ffn_down[l]
        x = x + gt2[:, None, :] * f
    return rmsnorm(x, g_norm_final)
```

```python
import functools

import jax
import jax.numpy as jnp
from jax import lax
from jax.experimental import pallas as pl
from jax.experimental.pallas import tpu as pltpu

D_MODEL = 2048
D_LRU = 1024
D_POOL = 1024
LRU_HEADS = 16
LRU_HEAD_DIM = 64
CONV_WIDTH = 4
LRU_C = 8.0
POOL_WINDOWS = (2, 4, 8, 16)
POOL_GROUP_DIM = 256
D_FF = 5632
N_MOD = 6
EPS = 1e-6

GROUP = 256
N_GROUPS = D_LRU // GROUP
CONV_HALO = 8
POOL_HALO = 16

F32 = jnp.float32
BF16 = jnp.bfloat16

VMEM_LIMIT = 56 * 1024 * 1024


def _rms(x, g):
    ms = jnp.mean(x * x, axis=-1, keepdims=True)
    return x * lax.rsqrt(ms + EPS) * g


def _mod_kernel(c_ref, w_ref, b_ref, o_ref):
    c = c_ref[...]
    ca = c * jax.nn.sigmoid(c)
    o_ref[...] = jnp.dot(ca, w_ref[...], preferred_element_type=F32,
                         precision=lax.Precision.HIGHEST) + b_ref[...]


def _mod_call(c_pad, w_ada, b_ada):
    n = w_ada.shape[1]
    tn = 1024
    return pl.pallas_call(
        _mod_kernel,
        grid=(n // tn,),
        in_specs=[
            pl.BlockSpec((8, D_MODEL), lambda j: (0, 0)),
            pl.BlockSpec((D_MODEL, tn), lambda j: (0, j)),
            pl.BlockSpec((1, tn), lambda j: (0, j)),
        ],
        out_specs=pl.BlockSpec((8, tn), lambda j: (0, j)),
        out_shape=jax.ShapeDtypeStruct((8, n), F32),
        compiler_params=pltpu.CompilerParams(
            dimension_semantics=("arbitrary",), vmem_limit_bytes=VMEM_LIMIT),
        name="adaln_mod",
    )(c_pad, w_ada, b_ada)


def _inproj_kernel(x_ref, g_ref, sh_ref, sc_ref, w_ref, o_ref):
    x = x_ref[0]
    h = _rms(x, g_ref[...]) * (1.0 + sc_ref[0]) + sh_ref[0]
    o_ref[0] = jnp.dot(h.astype(BF16), w_ref[...], preferred_element_type=F32)


def _inproj_call(x, g, sh, sc, w_in_b, tm):
    b, s, d = x.shape
    n = w_in_b.shape[1]
    return pl.pallas_call(
        _inproj_kernel,
        grid=(b, s // tm),
        in_specs=[
            pl.BlockSpec((1, tm, d), lambda i, j: (i, j, 0)),
            pl.BlockSpec((1, d), lambda i, j: (0, 0)),
            pl.BlockSpec((1, 1, d), lambda i, j: (i, 0, 0)),
            pl.BlockSpec((1, 1, d), lambda i, j: (i, 0, 0)),
            pl.BlockSpec((d, n), lambda i, j: (0, 0), pipeline_mode=pl.Buffered(1)),
        ],
        out_specs=pl.BlockSpec((1, tm, n), lambda i, j: (i, j, 0)),
        out_shape=jax.ShapeDtypeStruct((b, s, n), F32),
        compiler_params=pltpu.CompilerParams(
            dimension_semantics=("arbitrary", "arbitrary"), vmem_limit_bytes=VMEM_LIMIT),
        name="inproj",
    )(x, g, sh, sc, w_in_b)


def _shift_rows(v, k, fill):
    rows = lax.broadcasted_iota(jnp.int32, v.shape, 0)
    return jnp.where(rows >= k, pltpu.roll(v, k, 0), fill)


def _mixer_kernel(proj_ref, x_ref, gt_ref, wconv_ref, bconv_ref, wgate_ref, ba_ref, bi_ref,
                  lam_ref, wpool_ref, ls_ref, wout_ref, o_ref,
                  xr_ext, xp_ext, h_carry, ycat, *, tm):
    j = pl.program_id(1)

    @pl.when(j == 0)
    def _():
        xr_ext[0:CONV_HALO, :] = jnp.zeros((CONV_HALO, D_LRU), F32)
        xp_ext[0:POOL_HALO, :] = jnp.zeros((POOL_HALO, D_POOL), F32)
        h_carry[...] = jnp.zeros(h_carry.shape, F32)

    @pl.when(j != 0)
    def _():
        xr_ext[0:CONV_HALO, :] = xr_ext[tm:tm + CONV_HALO, :]
        xp_ext[0:POOL_HALO, :] = xp_ext[tm:tm + POOL_HALO, :]

    xr_ext[CONV_HALO:CONV_HALO + tm, :] = proj_ref[0, :, 0:D_LRU]
    xp_ext[POOL_HALO:POOL_HALO + tm, :] = proj_ref[0, :, 2 * D_LRU:2 * D_LRU + D_POOL]

    lam = lam_ref[...]
    log_sig = jnp.minimum(lam, 0.0) - jnp.log1p(jnp.exp(-jnp.abs(lam)))
    c_log = LRU_C * log_sig

    for g in range(N_GROUPS):
        cs = slice(g * GROUP, (g + 1) * GROUP)
        xe = xr_ext[:, cs]
        wc = wconv_ref[:, cs]
        conv = xe * wc[3:4, :]
        for k in range(1, CONV_WIDTH):
            conv = conv + pltpu.roll(xe, k, 0) * wc[3 - k:4 - k, :]
        xc = conv[CONV_HALO:, :] + bconv_ref[:, cs]

        gates = jnp.dot(xc.astype(BF16), wgate_ref[g], preferred_element_type=F32)
        r = jax.nn.sigmoid(gates[:, :GROUP] + ba_ref[:, cs])
        ig = jax.nn.sigmoid(gates[:, GROUP:] + bi_ref[:, cs])
        a = jnp.exp(r * c_log[:, cs])
        mult = jnp.sqrt(1.0 - a * a)
        u = mult * (ig * xc)

        k = 1
        while k < tm:
            u = a * _shift_rows(u, k, 0.0) + u
            a = a * _shift_rows(a, k, 1.0)
            k *= 2
        h = a * h_carry[0:1, cs] + u
        h_carry[0:1, cs] = h[tm - 1:tm, :]

        gr = proj_ref[0, :, D_LRU + g * GROUP:D_LRU + (g + 1) * GROUP]
        ycat[:, cs] = (h * jax.nn.gelu(gr, approximate=True)).astype(BF16)

    rows = lax.broadcasted_iota(jnp.int32, (tm, 1), 0) + j * tm + 1
    for g, w in enumerate(POOL_WINDOWS):
        cs = slice(g * GROUP, (g + 1) * GROUP)
        xe = xp_ext[:, cs]
        acc = xe
        k = 1
        while k < w:
            acc = acc + pltpu.roll(acc, k, 0)
            k *= 2
        xt = xe[POOL_HALO:, :]
        cnt = jnp.minimum(rows, w).astype(F32)
        pooled = acc[POOL_HALO:, :] / cnt - xt
        yp = jnp.dot(pooled.astype(BF16), wpool_ref[g], preferred_element_type=F32)
        ycat[:, D_LRU + g * GROUP:D_LRU + (g + 1) * GROUP] = (yp * ls_ref[:, cs]).astype(BF16)

    y = jnp.dot(ycat[...], wout_ref[...], preferred_element_type=F32)
    o_ref[0] = x_ref[0] + gt_ref[0] * y


def _mixer_call(proj, x, gt, wconv, bconv, wgate_b, ba, bi, lam, wpool_b, ls, wout_b, tm):
    b, s, d = x.shape
    n = proj.shape[-1]
    const2 = lambda i, j: (0, 0)
    const3 = lambda i, j: (0, 0, 0)
    return pl.pallas_call(
        functools.partial(_mixer_kernel, tm=tm),
        grid=(b, s // tm),
        in_specs=[
            pl.BlockSpec((1, tm, n), lambda i, j: (i, j, 0)),
            pl.BlockSpec((1, tm, d), lambda i, j: (i, j, 0)),
            pl.BlockSpec((1, 1, d), lambda i, j: (i, 0, 0)),
            pl.BlockSpec((CONV_WIDTH, D_LRU), const2),
            pl.BlockSpec((1, D_LRU), const2),
            pl.BlockSpec((N_GROUPS, GROUP, 2 * GROUP), const3),
            pl.BlockSpec((1, D_LRU), const2),
            pl.BlockSpec((1, D_LRU), const2),
            pl.BlockSpec((1, D_LRU), const2),
            pl.BlockSpec((N_GROUPS, GROUP, GROUP), const3),
            pl.BlockSpec((1, D_POOL), const2),
            pl.BlockSpec((d, d), const2, pipeline_mode=pl.Buffered(1)),
        ],
        out_specs=pl.BlockSpec((1, tm, d), lambda i, j: (i, j, 0)),
        out_shape=jax.ShapeDtypeStruct((b, s, d), F32),
        scratch_shapes=[
            pltpu.VMEM((CONV_HALO + tm, D_LRU), F32),
            pltpu.VMEM((POOL_HALO + tm, D_POOL), F32),
            pltpu.VMEM((8, D_LRU), F32),
            pltpu.VMEM((tm, d), BF16),
        ],
        compiler_params=pltpu.CompilerParams(
            dimension_semantics=("arbitrary", "arbitrary"), vmem_limit_bytes=VMEM_LIMIT),
        name="mixer",
    )(proj, x, gt, wconv, bconv, wgate_b, ba, bi, lam, wpool_b, ls, wout_b)


def _ffn_kernel(x_ref, g_ref, sh_ref, sc_ref, gt_ref, gf_ref, wg_ref, wu_ref, wd_ref, o_ref,
                h_scr, acc):
    j = pl.program_id(2)

    @pl.when(j == 0)
    def _():
        h = _rms(x_ref[0], g_ref[...]) * (1.0 + sc_ref[0]) + sh_ref[0]
        h_scr[...] = h.astype(BF16)

    h = h_scr[...]
    gate = jnp.dot(h, wg_ref[...], preferred_element_type=F32)
    up = jnp.dot(h, wu_ref[...], preferred_element_type=F32)
    act = (gate * jax.nn.sigmoid(gate) * up).astype(BF16)
    part = jnp.dot(act, wd_ref[...], preferred_element_type=F32)

    @pl.when(j == 0)
    def _():
        acc[...] = part

    @pl.when(j != 0)
    def _():
        acc[...] += part

    @pl.when(j == pl.num_programs(2) - 1)
    def _():
        o_ref[0] = _rms(x_ref[0] + gt_ref[0] * acc[...], gf_ref[...])


def _ffn_call(x1, g, sh, sc, gt, gf, wg_b, wu_b, wd_b, tm, tf):
    b, s, d = x1.shape
    f = wg_b.shape[1]
    return pl.pallas_call(
        _ffn_kernel,
        grid=(b, s // tm, f // tf),
        in_specs=[
            pl.BlockSpec((1, tm, d), lambda i, t, j: (i, t, 0)),
            pl.BlockSpec((1, d), lambda i, t, j: (0, 0)),
            pl.BlockSpec((1, 1, d), lambda i, t, j: (i, 0, 0)),
            pl.BlockSpec((1, 1, d), lambda i, t, j: (i, 0, 0)),
            pl.BlockSpec((1, 1, d), lambda i, t, j: (i, 0, 0)),
            pl.BlockSpec((1, d), lambda i, t, j: (0, 0)),
            pl.BlockSpec((d, tf), lambda i, t, j: (0, j)),
            pl.BlockSpec((d, tf), lambda i, t, j: (0, j)),
            pl.BlockSpec((tf, d), lambda i, t, j: (j, 0)),
        ],
        out_specs=pl.BlockSpec((1, tm, d), lambda i, t, j: (i, t, 0)),
        out_shape=jax.ShapeDtypeStruct((b, s, d), F32),
        scratch_shapes=[
            pltpu.VMEM((tm, d), BF16),
            pltpu.VMEM((tm, d), F32),
        ],
        compiler_params=pltpu.CompilerParams(
            dimension_semantics=("arbitrary", "arbitrary", "arbitrary"),
            vmem_limit_bytes=VMEM_LIMIT),
        name="ffn",
    )(x1, g, sh, sc, gt, gf, wg_b, wu_b, wd_b)


def _block_diag_groups(w):
    hpg = GROUP // LRU_HEAD_DIM
    w4 = w.reshape(N_GROUPS, hpg, LRU_HEAD_DIM, LRU_HEAD_DIM)
    eye = jnp.eye(hpg, dtype=w.dtype)
    bd = jnp.einsum("ghij,hk->ghikj", w4, eye)
    return bd.reshape(N_GROUPS, GROUP, GROUP)


def kernel(x, c, w_ada, b_ada, g_norm_mix, w_in, w_conv, b_conv, w_rg_a, b_rg_a, w_rg_i, b_rg_i,
           lru_lambda, w_pool, ls_pool, w_out, g_norm_ffn, w_ffn_gate, w_ffn_up, w_ffn_down,
           g_norm_final):
    b, s, d = x.shape
    l = 0
    c_pad = jnp.zeros((8, d), F32).at[:b].set(c)
    mod = _mod_call(c_pad, w_ada[l], b_ada[l].reshape(1, -1))[:b]
    sh1, sc1, gt1, sh2, sc2, gt2 = [m.reshape(b, 1, d) for m in jnp.split(mod, N_MOD, axis=-1)]

    proj = _inproj_call(x, g_norm_mix[l].reshape(1, d), sh1, sc1, w_in[l].astype(BF16), tm=512)

    wgate_b = jnp.concatenate(
        [_block_diag_groups(w_rg_a[l]), _block_diag_groups(w_rg_i[l])], axis=-1).astype(BF16)
    x1 = _mixer_call(
        proj, x, gt1, w_conv[l], b_conv[l].reshape(1, -1), wgate_b,
        b_rg_a[l].reshape(1, -1), b_rg_i[l].reshape(1, -1), lru_lambda[l].reshape(1, -1),
        w_pool[l].astype(BF16), ls_pool[l].reshape(1, -1), w_out[l].astype(BF16), tm=256)

    return _ffn_call(
        x1, g_norm_ffn[l].reshape(1, d), sh2, sc2, gt2, g_norm_final.reshape(1, d),
        w_ffn_gate[l].astype(BF16), w_ffn_up[l].astype(BF16), w_ffn_down[l].astype(BF16),
        tm=512, tf=512)
```
